```python
import math
import jax, jax.numpy as jnp
from jax import lax
import numpy as np

D_MODEL = 1024
BATCH = 2
SEQ = 8192
DEPTH = 2
DEC_BATCH = 32
DEC_SEQ = 8
PAST_LEN = 16384
PAGE_SIZE = 128

SSM_WIDTH = D_MODEL // 2
SSM_GROUP = 16
SSM_GROUPS = SSM_WIDTH // SSM_GROUP
SSM_STATE = 64
ATT_WIDTH = D_MODEL // 2
ATT_HEAD_DIM = 64
ATT_HEADS = ATT_WIDTH // ATT_HEAD_DIM
DILATED_PATTERNS = ((128, 1), (512, 4), (2048, 16))
WIN_MAX = 2048
N_BUCKETS = 32
BUCKET_MAX_DIST = 2048
RET_HEADS = 4
RET_KEY_DIM = D_MODEL // RET_HEADS
RET_VAL_DIM = 2 * RET_KEY_DIM
RET_VAL_WIDTH = RET_HEADS * RET_VAL_DIM
RET_CHUNK = 128
ROPE_BASE = 10000.0
GN_EPS = 1e-5
FFN_DIM = 2816
N_EXPERTS = 8
TOP_K = 2
EXPERT_DIM = 3584
EPS = 1e-6

kernel_name = 'hybrid_s5_dilated_retention_decoder_step'


def _rmsnorm(x, g):
    xf = x.astype(jnp.float32)
    y = xf * lax.rsqrt(jnp.mean(xf * xf, axis=-1, keepdims=True) + EPS)
    return (y * g.astype(jnp.float32)).astype(x.dtype)


def _sandwich(x, c, mod_w, mod_b, g_pre, g_post, fn):
    shift, scale, gate = jnp.split((jax.nn.silu(c) @ mod_w + mod_b)[:, None, :], 3, axis=-1)
    y, aux = fn(_rmsnorm(x, g_pre) * (1 + scale) + shift)
    return x + gate * _rmsnorm(y, g_post), aux


def _swiglu(h, w1, w3, w2):
    return (jax.nn.silu(h @ w1) * (h @ w3)) @ w2


def _moe(h, w_router, b_router, w1, w3, w2):
    logits = (h @ w_router).astype(jnp.float32) + b_router.astype(jnp.float32)
    top_val, top_idx = lax.top_k(logits, TOP_K)
    top_w = jax.nn.softmax(top_val, axis=-1)
    gates = jnp.sum(jax.nn.one_hot(top_idx, N_EXPERTS, dtype=jnp.float32) * top_w[..., None], axis=-2)
    out = jnp.zeros(h.shape, jnp.float32)
    for e in range(N_EXPERTS):
        out = out + gates[..., e:e + 1] * _swiglu(h, w1[e], w3[e], w2[e]).astype(jnp.float32)
    return out.astype(h.dtype)


def _t5_bucket(dist):
    dist = np.asarray(dist)
    exact = N_BUCKETS // 2
    safe = np.maximum(dist, 1).astype(np.float32)
    large = exact + (np.log(safe / exact) / np.log(BUCKET_MAX_DIST / exact) * (N_BUCKETS - exact)).astype(np.int32)
    return np.where(dist < exact, dist, np.minimum(large, N_BUCKETS - 1)).astype(np.int32)


def _softmax_lse(logits, mask):
    logits = jnp.where(mask, logits, -jnp.inf)
    m = jnp.max(logits, axis=-1, keepdims=True)
    p = jnp.exp(logits - m)
    s = jnp.sum(p, axis=-1, keepdims=True)
    return p / s, (m + jnp.log(s))[..., 0]


def _dilated_prompt(q, k, v, rel_table):
    bsz, s, h, e = q.shape
    outs, lses = [], []
    for window, dil in DILATED_PATTERNS:
        n = window // dil
        span = n * dil
        s_pad = -(-s // span) * span
        nblk = s_pad // span

        def blocks(a):
            a = jnp.pad(a, ((0, 0), (0, s_pad - s), (0, 0), (0, 0)))
            return a.reshape(bsz, nblk, n, dil, h, e)

        def with_prev(a):
            prev = jnp.pad(a, ((0, 0), (1, 0), (0, 0), (0, 0), (0, 0), (0, 0)))[:, :-1]
            return jnp.concatenate([prev, a], axis=2)

        qb = blocks(q)
        kk = with_prev(blocks(k))
        vv = with_prev(blocks(v))
        qi = np.arange(n)[:, None]
        ki = np.arange(2 * n)[None, :]
        back = qi + n - ki
        band = (back >= 0) & (back <= n)
        bias = jnp.transpose(rel_table[_t5_bucket(np.clip(back, 0, n) * dil)], (2, 0, 1)).astype(jnp.float32)
        first = (np.arange(nblk)[:, None, None] * n + ki[None] - n) >= 0
        mask = jnp.asarray(band[None] & first)[None, :, None, None]
        logits = jnp.einsum('zbqrhe,zbkrhe->zbrhqk', qb, kk).astype(jnp.float32) * e ** -0.5 + bias
        p, lse = _softmax_lse(logits, mask)
        o = jnp.einsum('zbrhqk,zbkrhe->zbqrhe', p, vv.astype(jnp.float32))
        outs.append(o.reshape(bsz, s_pad, h, e)[:, :s])
        lses.append(jnp.transpose(lse, (0, 1, 4, 2, 3)).reshape(bsz, s_pad, h)[:, :s])
    wts = jax.nn.softmax(jnp.stack(lses), axis=0)
    return jnp.einsum('pnsh,pnshe->nshe', wts, jnp.stack(outs))


def _dilated_sample(q, k_all, v_all, rel_table):
    bsz, t, h, e = q.shape
    w = k_all.shape[1] - t
    outs, lses = [], []
    for window, dil in DILATED_PATTERNS:
        n = window // dil
        dist = np.arange(n + 1) * dil
        idx = w + np.arange(t)[:, None] - dist[None, :]
        valid = jnp.asarray(idx >= 0)
        idx = jnp.asarray(np.maximum(idx, 0))
        kg = k_all[:, idx]
        vg = v_all[:, idx]
        bias = rel_table[_t5_bucket(dist)].astype(jnp.float32).T
        logits = jnp.einsum('nthe,ntjhe->nhtj', q, kg).astype(jnp.float32) * e ** -0.5 + bias[None, :, None, :]
        p, lse = _softmax_lse(logits, valid[None, None])
        outs.append(jnp.einsum('nhtj,ntjhe->nthe', p, vg.astype(jnp.float32)))
        lses.append(jnp.transpose(lse, (0, 2, 1)))
    wts = jax.nn.softmax(jnp.stack(lses), axis=0)
    return jnp.einsum('pnth,pnthe->nthe', wts, jnp.stack(outs))


def _cplx_combine(e1, e2):
    a1r, a1i, b1r, b1i = e1
    a2r, a2i, b2r, b2i = e2
    return (a2r * a1r - a2i * a1i, a2r * a1i + a2i * a1r,
            a2r * b1r - a2i * b1i + b2r, a2r * b1i + a2i * b1r + b2i)


def _s5(u, h0_re, h0_im, a_re, a_im, log_dt, b_re, b_im, c_re, c_im, d_skip, glu_w, glu_b):
    f32 = jnp.float32
    n, t, _ = u.shape
    ug = u.astype(f32).reshape(n, t, SSM_GROUPS, SSM_GROUP)
    a_re = a_re.astype(f32)
    a_im = a_im.astype(f32)
    dt = jnp.exp(log_dt.astype(f32))[:, None]
    mag = jnp.exp(a_re * dt)
    abar_re = mag * jnp.cos(a_im * dt)
    abar_im = mag * jnp.sin(a_im * dt)
    den = a_re * a_re + a_im * a_im
    f_re = ((abar_re - 1.0) * a_re + abar_im * a_im) / den
    f_im = (abar_im * a_re - (abar_re - 1.0) * a_im) / den
    b_re = b_re.astype(f32)
    b_im = b_im.astype(f32)
    bb_re = f_re[..., None] * b_re - f_im[..., None] * b_im
    bb_im = f_re[..., None] * b_im + f_im[..., None] * b_re
    bu_re = jnp.einsum('ntgi,gpi->ntgp', ug, bb_re)
    bu_im = jnp.einsum('ntgi,gpi->ntgp', ug, bb_im)
    ar = jnp.broadcast_to(abar_re, bu_re.shape)
    ai = jnp.broadcast_to(abar_im, bu_re.shape)
    ar, ai, xr, xi = lax.associative_scan(_cplx_combine, (ar, ai, bu_re, bu_im), axis=1)
    h0r = h0_re.astype(f32)[:, None]
    h0i = h0_im.astype(f32)[:, None]
    xr, xi = xr + ar * h0r - ai * h0i, xi + ar * h0i + ai * h0r
    y = (jnp.einsum('ntgp,gip->ntgi', xr, c_re.astype(f32)) - jnp.einsum('ntgp,gip->ntgi', xi, c_im.astype(f32))
         + d_skip.astype(f32) * ug).reshape(n, t, SSM_WIDTH)
    g = jax.nn.gelu(y)
    out = g * jax.nn.sigmoid(g @ glu_w.astype(f32) + glu_b.astype(f32))
    return out, xr[:, -1], xi[:, -1]


def _even_mixer(h, k_buf, v_buf, h0_re, h0_im, rel_table, w_in, w_out, ssm_params):
    n, t, _ = h.shape
    z = h @ w_in
    u = z[..., :SSM_WIDTH]
    q, k, v = [a.reshape(n, t, ATT_HEADS, ATT_HEAD_DIM) for a in jnp.split(z[..., SSM_WIDTH:], 3, axis=-1)]
    ssm_out, s_re, s_im = _s5(u, h0_re, h0_im, *ssm_params)
    if k_buf is None:
        att = _dilated_prompt(q, k, v, rel_table)
        keep = min(WIN_MAX, t)
        new_k, new_v = k[:, t - keep:], v[:, t - keep:]
    else:
        k_all = jnp.concatenate([k_buf.astype(k.dtype), k], axis=1)
        v_all = jnp.concatenate([v_buf.astype(v.dtype), v], axis=1)
        att = _dilated_sample(q, k_all, v_all, rel_table)
        new_k, new_v = k, v
    mixed = jnp.concatenate([ssm_out.astype(h.dtype), att.reshape(n, t, ATT_WIDTH).astype(h.dtype)], axis=-1)
    return mixed @ w_out, (new_k, new_v, s_re, s_im)


def _rope(x, pos):
    half = x.shape[-1] // 2
    inv = ROPE_BASE ** (-jnp.arange(half, dtype=jnp.float32) / half)
    ang = pos[:, None] * inv[None, :]
    cos = jnp.cos(ang)[None, :, None, :]
    sin = jnp.sin(ang)[None, :, None, :]
    x1, x2 = x[..., :half], x[..., half:]
    return jnp.concatenate([x1 * cos - x2 * sin, x1 * sin + x2 * cos], axis=-1)


def _retention(q, k, v, s0):
    f32 = jnp.float32
    n, t, h, _ = q.shape
    c = RET_CHUNK if t % RET_CHUNK == 0 else t
    nc = t // c
    log_g = jnp.log(1.0 - 2.0 ** (-5.0 - jnp.arange(h, dtype=f32)))
    i = jnp.arange(c, dtype=f32)
    rel = i[:, None] - i[None, :]
    intra_decay = jnp.where(rel >= 0, jnp.exp(jnp.maximum(rel, 0.0)[None] * log_g[:, None, None]), 0.0)
    q_decay = jnp.exp((i + 1.0)[:, None] * log_g[None, :])[None, :, :, None]
    k_decay = jnp.exp((c - 1.0 - i)[:, None] * log_g[None, :])[None, :, :, None]
    chunk_decay = jnp.exp(c * log_g)[None, :, None, None]

    def to_chunks(a):
        return jnp.moveaxis(a.astype(f32).reshape(n, nc, c, h, a.shape[-1]), 1, 0)

    def step(s, blk):
        qc, kc, vc = blk
        scores = jnp.einsum('nqhd,nkhd->nhqk', qc, kc) * intra_decay
        o = jnp.einsum('nhqk,nkhv->nqhv', scores, vc) + jnp.einsum('nqhd,nhdv->nqhv', qc, s) * q_decay
        s = chunk_decay * s + jnp.einsum('nkhd,nkhv->nhdv', kc * k_decay, vc)
        return s, o

    s_fin, o = lax.scan(step, s0.astype(f32), (to_chunks(q), to_chunks(k), to_chunks(v)))
    return jnp.moveaxis(o, 0, 1).reshape(n, t, h, -1), s_fin


def _odd_mixer(h, s0, pos0, w_in, w_out):
    f32 = jnp.float32
    n, t, _ = h.shape
    q, k, v, g = jnp.split(h @ w_in, [D_MODEL, 2 * D_MODEL, 2 * D_MODEL + RET_VAL_WIDTH], axis=-1)
    pos = pos0 + jnp.arange(t, dtype=f32)
    q = _rope(q.reshape(n, t, RET_HEADS, RET_KEY_DIM).astype(f32), pos)
    k = _rope(k.reshape(n, t, RET_HEADS, RET_KEY_DIM).astype(f32), pos) * RET_KEY_DIM ** -0.5
    v = v.reshape(n, t, RET_HEADS, RET_VAL_DIM).astype(f32)
    o, s_new = _retention(q, k, v, s0)
    mu = jnp.mean(o, axis=-1, keepdims=True)
    var = jnp.mean(jnp.square(o - mu), axis=-1, keepdims=True)
    o = (o - mu) * lax.rsqrt(var + GN_EPS)
    y = (jax.nn.silu(g) * o.reshape(n, t, RET_VAL_WIDTH).astype(h.dtype)) @ w_out
    return y, s_new


def setup_inputs(seed: int = 0) -> dict:
    key = jax.random.key(seed)
    keys = iter(jax.random.split(key, 64))
    f32 = jnp.float32
    D = D_MODEL

    def nrm(shape, scale):
        return jax.random.normal(next(keys), shape, f32) * scale

    def gain():
        return 1.0 + nrm((D,), 0.01)

    win = min(WIN_MAX, PAST_LEN)
    p = {}
    p['x_prompt'] = nrm((BATCH, SEQ, D), 1.0)
    p['x_sample'] = nrm((DEC_BATCH, DEC_SEQ, D), 1.0)
    p['cache_k_win'] = nrm((DEC_BATCH, win, ATT_HEADS, ATT_HEAD_DIM), 1.0)
    p['cache_v_win'] = nrm((DEC_BATCH, win, ATT_HEADS, ATT_HEAD_DIM), 1.0)
    p['state_ssm_re'] = nrm((DEC_BATCH, SSM_GROUPS, SSM_STATE), 0.1)
    p['state_ssm_im'] = nrm((DEC_BATCH, SSM_GROUPS, SSM_STATE), 0.1)
    p['state_ret'] = nrm((DEC_BATCH, RET_HEADS, RET_KEY_DIM, RET_VAL_DIM), 0.25)
    p['c_prompt'] = nrm((BATCH, D), 1.0)
    p['c_sample'] = nrm((DEC_BATCH, D), 1.0)
    p['rel_bias_table'] = nrm((N_BUCKETS, ATT_HEADS), 0.1)
    p['l0_mix_mod_w'] = nrm((D, 3 * D), 0.5 * D ** -0.5)
    p['l0_mix_mod_b'] = nrm((3 * D,), 0.01)
    p['l0_mix_norm_pre'] = gain()
    p['l0_mix_norm_post'] = gain()
    p['l0_w_in'] = nrm((D, SSM_WIDTH + 3 * ATT_WIDTH), D ** -0.5)
    p['l0_w_out'] = nrm((SSM_WIDTH + ATT_WIDTH, D), (SSM_WIDTH + ATT_WIDTH) ** -0.5)
    p['l0_ssm_a_re'] = -0.5 + nrm((SSM_GROUPS, SSM_STATE), 0.01)
    p['l0_ssm_a_im'] = math.pi * jnp.arange(SSM_STATE, dtype=f32)[None, :] + nrm((SSM_GROUPS, SSM_STATE), 0.01)
    p['l0_ssm_log_dt'] = jax.random.uniform(next(keys), (SSM_GROUPS,), f32, math.log(1e-3), math.log(1e-1))
    p['l0_ssm_b_re'] = nrm((SSM_GROUPS, SSM_STATE, SSM_GROUP), (2 * SSM_GROUP) ** -0.5)
    p['l0_ssm_b_im'] = nrm((SSM_GROUPS, SSM_STATE, SSM_GROUP), (2 * SSM_GROUP) ** -0.5)
    p['l0_ssm_c_re'] = nrm((SSM_GROUPS, SSM_GROUP, SSM_STATE), (2 * SSM_STATE) ** -0.5)
    p['l0_ssm_c_im'] = nrm((SSM_GROUPS, SSM_GROUP, SSM_STATE), (2 * SSM_STATE) ** -0.5)
    p['l0_ssm_d'] = nrm((SSM_GROUPS, SSM_GROUP), 1.0)
    p['l0_glu_w'] = nrm((SSM_WIDTH, SSM_WIDTH), SSM_WIDTH ** -0.5)
    p['l0_glu_b'] = nrm((SSM_WIDTH,), 0.01)
    p['l0_ffn_mod_w'] = nrm((D, 3 * D), 0.5 * D ** -0.5)
    p['l0_ffn_mod_b'] = nrm((3 * D,), 0.01)
    p['l0_ffn_norm_pre'] = gain()
    p['l0_ffn_norm_post'] = gain()
    p['l0_ffn_w1'] = nrm((D, FFN_DIM), D ** -0.5)
    p['l0_ffn_w3'] = nrm((D, FFN_DIM), D ** -0.5)
    p['l0_ffn_w2'] = nrm((FFN_DIM, D), FFN_DIM ** -0.5)
    p['l1_mix_mod_w'] = nrm((D, 3 * D), 0.5 * D ** -0.5)
    p['l1_mix_mod_b'] = nrm((3 * D,), 0.01)
    p['l1_mix_norm_pre'] = gain()
    p['l1_mix_norm_post'] = gain()
    p['l1_w_in'] = nrm((D, 2 * D + 2 * RET_VAL_WIDTH), D ** -0.5)
    p['l1_w_out'] = nrm((RET_VAL_WIDTH, D), RET_VAL_WIDTH ** -0.5)
    p['l1_ffn_mod_w'] = nrm((D, 3 * D), 0.5 * D ** -0.5)
    p['l1_ffn_mod_b'] = nrm((3 * D,), 0.01)
    p['l1_ffn_norm_pre'] = gain()
    p['l1_ffn_norm_post'] = gain()
    p['l1_router_w'] = nrm((D, N_EXPERTS), D ** -0.5)
    p['l1_router_b'] = nrm((N_EXPERTS,), 0.01)
    p['l1_moe_w1'] = nrm((N_EXPERTS, D, EXPERT_DIM), D ** -0.5)
    p['l1_moe_w3'] = nrm((N_EXPERTS, D, EXPERT_DIM), D ** -0.5)
    p['l1_moe_w2'] = nrm((N_EXPERTS, EXPERT_DIM, D), EXPERT_DIM ** -0.5)
    return p


def reference(x_prompt, x_sample, cache_k_win, cache_v_win, state_ssm_re, state_ssm_im, state_ret,
              c_prompt, c_sample, rel_bias_table,
              l0_mix_mod_w, l0_mix_mod_b, l0_mix_norm_pre, l0_mix_norm_post, l0_w_in, l0_w_out,
              l0_ssm_a_re, l0_ssm_a_im, l0_ssm_log_dt, l0_ssm_b_re, l0_ssm_b_im, l0_ssm_c_re, l0_ssm_c_im,
              l0_ssm_d, l0_glu_w, l0_glu_b,
              l0_ffn_mod_w, l0_ffn_mod_b, l0_ffn_norm_pre, l0_ffn_norm_post, l0_ffn_w1, l0_ffn_w3, l0_ffn_w2,
              l1_mix_mod_w, l1_mix_mod_b, l1_mix_norm_pre, l1_mix_norm_post, l1_w_in, l1_w_out,
              l1_ffn_mod_w, l1_ffn_mod_b, l1_ffn_norm_pre, l1_ffn_norm_post,
              l1_router_w, l1_router_b, l1_moe_w1, l1_moe_w3, l1_moe_w2):
    ssm_params = (l0_ssm_a_re, l0_ssm_a_im, l0_ssm_log_dt, l0_ssm_b_re, l0_ssm_b_im,
                  l0_ssm_c_re, l0_ssm_c_im, l0_ssm_d, l0_glu_w, l0_glu_b)
    yp, ys = x_prompt, x_sample
    for layer in range(DEPTH):
        if layer % 2 == 0:
            zero = jnp.zeros((yp.shape[0], SSM_GROUPS, SSM_STATE), jnp.float32)
            yp, (k_win_p, v_win_p, ssm_re_p, ssm_im_p) = _sandwich(
                yp, c_prompt, l0_mix_mod_w, l0_mix_mod_b, l0_mix_norm_pre, l0_mix_norm_post,
                lambda h: _even_mixer(h, None, None, zero, zero, rel_bias_table, l0_w_in, l0_w_out, ssm_params))
            ys, (k_new_s, v_new_s, ssm_re_s, ssm_im_s) = _sandwich(
                ys, c_sample, l0_mix_mod_w, l0_mix_mod_b, l0_mix_norm_pre, l0_mix_norm_post,
                lambda h: _even_mixer(h, cache_k_win, cache_v_win, state_ssm_re, state_ssm_im,
                                      rel_bias_table, l0_w_in, l0_w_out, ssm_params))
            yp, _ = _sandwich(yp, c_prompt, l0_ffn_mod_w, l0_ffn_mod_b, l0_ffn_norm_pre, l0_ffn_norm_post,
                              lambda h: (_swiglu(h, l0_ffn_w1, l0_ffn_w3, l0_ffn_w2), None))
            ys, _ = _sandwich(ys, c_sample, l0_ffn_mod_w, l0_ffn_mod_b, l0_ffn_norm_pre, l0_ffn_norm_post,
                              lambda h: (_swiglu(h, l0_ffn_w1, l0_ffn_w3, l0_ffn_w2), None))
        else:
            zero_ret = jnp.zeros((yp.shape[0], RET_HEADS, RET_KEY_DIM, RET_VAL_DIM), jnp.float32)
            yp, ret_p = _sandwich(yp, c_prompt, l1_mix_mod_w, l1_mix_mod_b, l1_mix_norm_pre, l1_mix_norm_post,
                                  lambda h: _odd_mixer(h, zero_ret, 0, l1_w_in, l1_w_out))
            ys, ret_s = _sandwich(ys, c_sample, l1_mix_mod_w, l1_mix_mod_b, l1_mix_norm_pre, l1_mix_norm_post,
                                  lambda h: _odd_mixer(h, state_ret, PAST_LEN, l1_w_in, l1_w_out))
            yp, _ = _sandwich(yp, c_prompt, l1_ffn_mod_w, l1_ffn_mod_b, l1_ffn_norm_pre, l1_ffn_norm_post,
                              lambda h: (_moe(h, l1_router_w, l1_router_b, l1_moe_w1, l1_moe_w3, l1_moe_w2), None))
            ys, _ = _sandwich(ys, c_sample, l1_ffn_mod_w, l1_ffn_mod_b, l1_ffn_norm_pre, l1_ffn_norm_post,
                              lambda h: (_moe(h, l1_router_w, l1_router_b, l1_moe_w1, l1_moe_w3, l1_moe_w2), None))
    return (yp, ys, k_win_p, v_win_p, ssm_re_p, ssm_im_p, ret_p, k_new_s, v_new_s, ssm_re_s, ssm_im_s, ret_s)
```

```python
import functools
import math

import jax
import jax.numpy as jnp
import numpy as np
from jax import lax
from jax.experimental import pallas as pl
from jax.experimental.pallas import tpu as pltpu

F32 = jnp.float32
BF16 = jnp.bfloat16

D_MODEL = 1024
EPS = 1e-6
GN_EPS = 1e-5
ROPE_BASE = 10000.0

SSM_WIDTH = 512
SSM_GROUP = 16
SSM_GROUPS = 32
SSM_STATE = 64
SLAB_ROWS = 8
SLAB_HALF = 256

ATT_WIDTH = 512
ATT_HEADS = 8
ATT_HEAD_DIM = 64
DILATED_PATTERNS = ((128, 1), (512, 4), (2048, 16))
ATT_STEPS = 128
WIN_MAX = 2048
N_BUCKETS = 32
BUCKET_MAX_DIST = 2048

RET_HEADS = 4
RET_KEY_DIM = 256
RET_VAL_DIM = 512
RET_VAL_WIDTH = 2048
RET_CHUNK = 128
PAST_POS = 16384

N_EXPERTS = 8
TOP_K = 2
MOE_TILE = 1024
MOE_SUB = 256
MOE_TN = 512

LANES = 128
VMEM_LIMIT = 56 * 1024 * 1024


def _cparams(sem):
    return pltpu.CompilerParams(dimension_semantics=sem, vmem_limit_bytes=VMEM_LIMIT)


def _dot(a, b):
    return jnp.dot(a, b, preferred_element_type=F32)


def _dot_nt(a, b):
    return lax.dot_general(a, b, (((1,), (1,)), ((), ())), preferred_element_type=F32)


def _dot_tn(a, b):
    return lax.dot_general(a, b, (((0,), (0,)), ((), ())), preferred_element_type=F32)


def _rms(x, g):
    return x * lax.rsqrt(jnp.mean(x * x, axis=-1, keepdims=True) + EPS) * g


def _mod_rows(ref, seq_mod):
    return ref[0] if seq_mod else ref[...]


def _mod_spec(seq_mod, tm, rows_per_seq, ngrid):
    d = D_MODEL
    if seq_mod:
        if ngrid == 1:
            return pl.BlockSpec((1, 1, d), lambda i: (i * tm // rows_per_seq, 0, 0))
        return pl.BlockSpec((1, 1, d), lambda i, j: (i * tm // rows_per_seq, 0, 0))
    if ngrid == 1:
        return pl.BlockSpec((tm, d), lambda i: (i, 0))
    return pl.BlockSpec((tm, d), lambda i, j: (i, 0))


def _adaln_kernel(c_ref, w_ref, b_ref, o_ref):
    s = jax.nn.silu(c_ref[...]).astype(BF16)
    o_ref[...] = _dot(s, w_ref[...].astype(BF16)) + b_ref[...]


def _adaln(c_all, w, b):
    r, d = c_all.shape
    n = w.shape[1]
    tn = 768
    return pl.pallas_call(
        _adaln_kernel,
        grid=(n // tn,),
        in_specs=[pl.BlockSpec((r, d), lambda j: (0, 0)),
                  pl.BlockSpec((d, tn), lambda j: (0, j)),
                  pl.BlockSpec((1, tn), lambda j: (0, j))],
        out_specs=pl.BlockSpec((r, tn), lambda j: (0, j)),
        out_shape=jax.ShapeDtypeStruct((r, n), F32),
        compiler_params=_cparams(("arbitrary",)),
        name="adaln",
    )(c_all, w, b.reshape(1, n))


def _lin_in_kernel(x_ref, sh_ref, sc_ref, g_ref, w_ref, o_ref, h_scr, *, seq_mod):
    @pl.when(pl.program_id(1) == 0)
    def _():
        h = _rms(x_ref[...], g_ref[...]) * (1.0 + _mod_rows(sc_ref, seq_mod)) + _mod_rows(sh_ref, seq_mod)
        h_scr[...] = h.astype(BF16)

    o_ref[...] = _dot(h_scr[...], w_ref[...].astype(BF16))


def _lin_in(x, shift, scale, g_pre, w, *, seq_mod, rows_per_seq, tm, tn):
    t, d = x.shape
    n = w.shape[1]
    ms = _mod_spec(seq_mod, tm, rows_per_seq, 2)
    return pl.pallas_call(
        functools.partial(_lin_in_kernel, seq_mod=seq_mod),
        grid=(t // tm, n // tn),
        in_specs=[pl.BlockSpec((tm, d), lambda i, j: (i, 0)), ms, ms,
                  pl.BlockSpec((1, d), lambda i, j: (0, 0)),
                  pl.BlockSpec((d, tn), lambda i, j: (0, j))],
        out_specs=pl.BlockSpec((tm, tn), lambda i, j: (i, j)),
        out_shape=jax.ShapeDtypeStruct((t, n), F32),
        scratch_shapes=[pltpu.VMEM((tm, d), BF16)],
        compiler_params=_cparams(("arbitrary", "arbitrary")),
        name="lin_in",
    )(x, shift, scale, g_pre.reshape(1, d), w)


def _lin_out_kernel(*refs, n_act, seq_mod):
    acts = refs[:n_act]
    ws = refs[n_act:2 * n_act]
    x_ref, gt_ref, g_ref, o_ref = refs[2 * n_act:2 * n_act + 4]
    w_scrs = refs[2 * n_act + 4:]

    @pl.when(pl.program_id(0) == 0)
    def _():
        for w_ref, w_scr in zip(ws, w_scrs):
            w_scr[...] = w_ref[...].astype(BF16)

    y = None
    for a_ref, w_scr in zip(acts, w_scrs):
        part = _dot(a_ref[...].astype(BF16), w_scr[...])
        y = part if y is None else y + part
    o_ref[...] = x_ref[...] + _mod_rows(gt_ref, seq_mod) * _rms(y, g_ref[...])


def _lin_out(acts, w_specs, w_arrays, x, gate, g_post, *, seq_mod, rows_per_seq, tm):
    t, d = x.shape
    n_act = len(acts)
    in_specs = [pl.BlockSpec((tm, a.shape[1]), lambda i: (i, 0)) for a in acts]
    in_specs += list(w_specs)
    in_specs += [pl.BlockSpec((tm, d), lambda i: (i, 0)),
                 _mod_spec(seq_mod, tm, rows_per_seq, 1),
                 pl.BlockSpec((1, d), lambda i: (0, 0))]
    scratch = [pltpu.VMEM(ws.block_shape, BF16) for ws in w_specs]
    return pl.pallas_call(
        functools.partial(_lin_out_kernel, n_act=n_act, seq_mod=seq_mod),
        grid=(t // tm,),
        in_specs=in_specs,
        out_specs=pl.BlockSpec((tm, d), lambda i: (i, 0)),
        out_shape=jax.ShapeDtypeStruct((t, d), F32),
        scratch_shapes=scratch,
        compiler_params=_cparams(("arbitrary",)),
        name="lin_out",
    )(*acts, *w_arrays, x, gate, g_post.reshape(1, d))


def _ffn_kernel(x_ref, sh_ref, sc_ref, gt_ref, gpre_ref, gpost_ref, w1_ref, w3_ref, w2_ref, o_ref,
                h_scr, acc_scr, *, seq_mod):
    j = pl.program_id(1)

    @pl.when(j == 0)
    def _():
        h = _rms(x_ref[...], gpre_ref[...]) * (1.0 + _mod_rows(sc_ref, seq_mod)) + _mod_rows(sh_ref, seq_mod)
        h_scr[...] = h.astype(BF16)

    h = h_scr[...]
    a = jax.nn.silu(_dot(h, w1_ref[...].astype(BF16))) * _dot(h, w3_ref[...].astype(BF16))
    part = _dot(a.astype(BF16), w2_ref[...].astype(BF16))

    @pl.when(j == 0)
    def _():
        acc_scr[...] = part

    @pl.when(j > 0)
    def _():
        acc_scr[...] += part

    @pl.when(j == pl.num_programs(1) - 1)
    def _():
        o_ref[...] = x_ref[...] + _mod_rows(gt_ref, seq_mod) * _rms(acc_scr[...], gpost_ref[...])


def _ffn(x, shift, scale, gate, g_pre, g_post, w1, w3, w2, *, seq_mod, rows_per_seq, tm, tn):
    t, d = x.shape
    f = w1.shape[1]
    ms = _mod_spec(seq_mod, tm, rows_per_seq, 2)
    vec = pl.BlockSpec((1, d), lambda i, j: (0, 0))
    return pl.pallas_call(
        functools.partial(_ffn_kernel, seq_mod=seq_mod),
        grid=(t // tm, f // tn),
        in_specs=[pl.BlockSpec((tm, d), lambda i, j: (i, 0)), ms, ms, ms, vec, vec,
                  pl.BlockSpec((d, tn), lambda i, j: (0, j)),
                  pl.BlockSpec((d, tn), lambda i, j: (0, j)),
                  pl.BlockSpec((tn, d), lambda i, j: (j, 0))],
        out_specs=pl.BlockSpec((tm, d), lambda i, j: (i, 0)),
        out_shape=jax.ShapeDtypeStruct((t, d), F32),
        scratch_shapes=[pltpu.VMEM((tm, d), BF16), pltpu.VMEM((tm, d), F32)],
        compiler_params=_cparams(("arbitrary", "arbitrary")),
        name="ffn",
    )(x, shift, scale, gate, g_pre.reshape(1, d), g_post.reshape(1, d), w1, w3, w2)


def _s5_kernel(u_ref, h0r_ref, h0i_ref, abar_ref, bw_ref, cw_ref, d_ref, gw_ref, gb_ref,
               o_ref, sr_ref, si_ref, bu_scr, st_scr, *, nb, tb):
    rows = nb * tb

    @pl.when(pl.program_id(1) == 0)
    def _():
        st_scr[:, :, 0:SLAB_HALF] = h0r_ref[...]
        st_scr[:, :, SLAB_HALF:] = h0i_ref[...]

    u = u_ref[...].reshape(rows, SSM_WIDTH)
    n_lt = 2 * SLAB_HALF // LANES
    for k in range(SLAB_ROWS):
        m = k // 2
        lhs = u[:, LANES * m:LANES * (m + 1)].astype(BF16)
        bu = _dot(lhs, bw_ref[k])
        for c in range(n_lt):
            bu_scr[c, pl.ds(k, rows, stride=SLAB_ROWS), :] = bu[:, LANES * c:LANES * (c + 1)]

    ar = abar_ref[:, 0:SLAB_HALF]
    ai = abar_ref[:, SLAB_HALF:]

    def step(t, carry):
        new = []
        for b in range(nb):
            xr, xi = carry[b]
            row = pl.ds(pl.multiple_of((b * tb + t) * SLAB_ROWS, SLAB_ROWS), SLAB_ROWS)
            br = jnp.concatenate([bu_scr[0, row, :], bu_scr[1, row, :]], axis=-1)
            bi = jnp.concatenate([bu_scr[2, row, :], bu_scr[3, row, :]], axis=-1)
            nr = ar * xr - ai * xi + br
            ni = ar * xi + ai * xr + bi
            bu_scr[0, row, :] = nr[:, :LANES]
            bu_scr[1, row, :] = nr[:, LANES:]
            bu_scr[2, row, :] = ni[:, :LANES]
            bu_scr[3, row, :] = ni[:, LANES:]
            new.append((nr, ni))
        return tuple(new)

    init = tuple((st_scr[b, :, 0:SLAB_HALF], st_scr[b, :, SLAB_HALF:]) for b in range(nb))
    fin = lax.fori_loop(0, tb, step, init, unroll=min(tb, 8))
    for b in range(nb):
        st_scr[b, :, 0:SLAB_HALF] = fin[b][0]
        st_scr[b, :, SLAB_HALF:] = fin[b][1]
    sr_ref[...] = st_scr[:, :, 0:SLAB_HALF]
    si_ref[...] = st_scr[:, :, SLAB_HALF:]

    def slab_row(k):
        return jnp.concatenate([bu_scr[c, pl.ds(k, rows, stride=SLAB_ROWS), :] for c in range(n_lt)],
                               axis=-1).astype(BF16)

    ys = []
    for m in range(SSM_WIDTH // LANES):
        acc = _dot(slab_row(2 * m), cw_ref[2 * m]) + _dot(slab_row(2 * m + 1), cw_ref[2 * m + 1])
        ys.append(acc + d_ref[:, LANES * m:LANES * (m + 1)] * u[:, LANES * m:LANES * (m + 1)])
    g = jax.nn.gelu(jnp.concatenate(ys, axis=-1))
    gate = jax.nn.sigmoid(_dot(g.astype(BF16), gw_ref[...]) + gb_ref[...])
    o_ref[...] = (g * gate).reshape(nb, tb, SSM_WIDTH)


def _s5_weights(a_re, a_im, log_dt, b_re, b_im, c_re, c_im):
    dt = jnp.exp(log_dt.astype(F32))[:, None]
    a_re = a_re.astype(F32)
    a_im = a_im.astype(F32)
    mag = jnp.exp(a_re * dt)
    abar_re = mag * jnp.cos(a_im * dt)
    abar_im = mag * jnp.sin(a_im * dt)
    den = a_re * a_re + a_im * a_im
    f_re = ((abar_re - 1.0) * a_re + abar_im * a_im) / den
    f_im = (abar_im * a_re - (abar_re - 1.0) * a_im) / den
    b_re = b_re.astype(F32)
    b_im = b_im.astype(F32)
    bb_re = f_re[..., None] * b_re - f_im[..., None] * b_im
    bb_im = f_re[..., None] * b_im + f_im[..., None] * b_re

    abar = jnp.concatenate([abar_re.reshape(SLAB_ROWS, SLAB_HALF), abar_im.reshape(SLAB_ROWS, SLAB_HALF)], axis=-1)

    eye4 = jnp.eye(4, dtype=F32)
    eye2 = jnp.eye(2, dtype=F32)
    khalf = eye2[np.arange(SLAB_ROWS) % 2]
    tb = jnp.stack([bb_re, bb_im], axis=0).reshape(2, SLAB_ROWS, 4, SSM_STATE, SSM_GROUP)
    bw = jnp.einsum('ckgpi,gh,kq->kqgichp', tb, eye4, khalf)
    bw = bw.reshape(SLAB_ROWS, LANES, 2 * SLAB_HALF).astype(BF16)
    tc = jnp.stack([c_re.astype(F32), -c_im.astype(F32)], axis=0).reshape(2, SLAB_ROWS, 4, SSM_GROUP, SSM_STATE)
    cw = jnp.einsum('ckgip,gh,kq->kcgpqhi', tc, eye4, khalf)
    cw = cw.reshape(SLAB_ROWS, 2 * SLAB_HALF, LANES).astype(BF16)
    return abar, bw, cw


def _s5(z3, h0_re, h0_im, abar, bw, cw, d_skip, glu_w, glu_b, *, nb, tb):
    n, t, _ = z3.shape
    const2 = lambda shape: pl.BlockSpec(shape, lambda bi, ti: (0,) * len(shape))
    rows = nb * tb
    return pl.pallas_call(
        functools.partial(_s5_kernel, nb=nb, tb=tb),
        grid=(n // nb, t // tb),
        in_specs=[pl.BlockSpec((nb, tb, SSM_WIDTH), lambda bi, ti: (bi, ti, 0)),
                  pl.BlockSpec((nb, SLAB_ROWS, SLAB_HALF), lambda bi, ti: (bi, 0, 0)),
                  pl.BlockSpec((nb, SLAB_ROWS, SLAB_HALF), lambda bi, ti: (bi, 0, 0)),
                  const2((SLAB_ROWS, 2 * SLAB_HALF)),
                  const2((SLAB_ROWS, LANES, 2 * SLAB_HALF)),
                  const2((SLAB_ROWS, 2 * SLAB_HALF, LANES)),
                  const2((1, SSM_WIDTH)),
                  const2((SSM_WIDTH, SSM_WIDTH)),
                  const2((1, SSM_WIDTH))],
        out_specs=[pl.BlockSpec((nb, tb, SSM_WIDTH), lambda bi, ti: (bi, ti, 0)),
                   pl.BlockSpec((nb, SLAB_ROWS, SLAB_HALF), lambda bi, ti: (bi, 0, 0)),
                   pl.BlockSpec((nb, SLAB_ROWS, SLAB_HALF), lambda bi, ti: (bi, 0, 0))],
        out_shape=[jax.ShapeDtypeStruct((n, t, SSM_WIDTH), F32),
                   jax.ShapeDtypeStruct((n, SLAB_ROWS, SLAB_HALF), F32),
                   jax.ShapeDtypeStruct((n, SLAB_ROWS, SLAB_HALF), F32)],
        scratch_shapes=[pltpu.VMEM((2 * SLAB_HALF // LANES, rows * SLAB_ROWS, LANES), F32),
                        pltpu.VMEM((nb, SLAB_ROWS, 2 * SLAB_HALF), F32)],
        compiler_params=_cparams(("arbitrary", "arbitrary")),
        name="s5",
    )(z3, h0_re, h0_im, abar, bw, cw, d_skip.reshape(1, SSM_WIDTH).astype(F32),
      glu_w.astype(BF16), glu_b.reshape(1, SSM_WIDTH).astype(F32))


def _t5_bucket(dist):
    dist = np.asarray(dist)
    exact = N_BUCKETS // 2
    safe = np.maximum(dist, 1).astype(np.float32)
    large = exact + (np.log(safe / exact) / np.log(BUCKET_MAX_DIST / exact) * (N_BUCKETS - exact)).astype(np.int32)
    return np.where(dist < exact, dist, np.minimum(large, N_BUCKETS - 1)).astype(np.int32)


def _prompt_bias(rel_table):
    n = ATT_STEPS
    qi = np.arange(n)[:, None]
    ki = np.arange(2 * n)[None, :]
    back = qi + n - ki
    band = (back >= 0) & (back <= n)
    out = []
    for _, dil in DILATED_PATTERNS:
        idx = _t5_bucket(np.clip(back, 0, n) * dil)
        bias = jnp.transpose(rel_table[idx], (2, 0, 1)).astype(F32)
        out.append(jnp.where(jnp.asarray(band)[None], bias, -jnp.inf))
    return jnp.stack(out)


def _att_kernel(q_ref, kc_ref, vc_ref, kp_ref, vp_ref, bias_ref, o_ref, kcat, vcat, o_s, lse_s, *, span):
    n = ATT_STEPS
    i = pl.program_id(2)
    kcat[0:span] = kp_ref[0]
    kcat[span:2 * span] = kc_ref[0]
    vcat[0:span] = vp_ref[0]
    vcat[span:2 * span] = vc_ref[0]
    head0 = lax.broadcasted_iota(jnp.int32, (n, LANES), 1) < ATT_HEAD_DIM
    prev_cols = lax.broadcasted_iota(jnp.int32, (n, 2 * n), 1) < n

    for pi, (_, dil) in enumerate(DILATED_PATTERNS):
        blk = n * dil

        def sub(sb, carry, pi=pi, dil=dil, blk=blk):
            lb = sb // dil
            base = lb * blk + sb % dil
            if dil == 1:
                base = pl.multiple_of(base, n)
                rows_q = pl.ds(base, n)
                rows_k = pl.ds(pl.multiple_of(span + base - blk, n), 2 * n)
            else:
                rows_q = pl.ds(base, n, stride=dil)
                rows_k = pl.ds(span + base - blk, 2 * n, stride=dil)
            qs = q_ref[0, rows_q, :] * (ATT_HEAD_DIM ** -0.5)
            kk = kcat[rows_k, :].astype(BF16)
            vv = vcat[rows_k, :].astype(BF16)
            neg = jnp.where(jnp.logical_or(i > 0, lb > 0), 0.0, -jnp.inf)
            edge = jnp.where(prev_cols, neg, 0.0)
            outs, lses = [], []
            for hh in range(2):
                sel = head0 if hh == 0 else jnp.logical_not(head0)
                qh = jnp.where(sel, qs, 0.0).astype(BF16)
                s = _dot_nt(qh, kk) + bias_ref[pi, hh] + edge
                m = jnp.max(s, axis=-1, keepdims=True)
                e = jnp.exp(s - m)
                l = jnp.sum(e, axis=-1, keepdims=True)
                outs.append(_dot((e / l).astype(BF16), vv))
                lses.append(jnp.broadcast_to(m + jnp.log(l), (n, LANES)))
            o_s[pi, rows_q, :] = jnp.where(head0, outs[0], outs[1])
            lse_s[pi, rows_q, :] = jnp.where(head0, lses[0], lses[1])
            return carry

        lax.fori_loop(0, span // n, sub, 0)

    lse = [lse_s[pi] for pi in range(len(DILATED_PATTERNS))]
    top = functools.reduce(jnp.maximum, lse)
    w = [jnp.exp(x - top) for x in lse]
    den = functools.reduce(lambda a, b: a + b, w)
    out = None
    for pi in range(len(DILATED_PATTERNS)):
        term = (w[pi] / den) * o_s[pi]
        out = term if out is None else out + term
    o_ref[0] = out.astype(o_ref.dtype)


def _att_prompt(z3, bias, *, span):
    n, s, _ = z3.shape
    qcol, kcol, vcol = 4, 8, 12
    cur = lambda off: pl.BlockSpec((1, span, LANES), lambda b, hp, i: (b, i, off + hp))
    prev = lambda off: pl.BlockSpec((1, span, LANES), lambda b, hp, i: (b, jnp.maximum(i - 1, 0), off + hp))
    return pl.pallas_call(
        functools.partial(_att_kernel, span=span),
        grid=(n, ATT_WIDTH // LANES, s // span),
        in_specs=[cur(qcol), cur(kcol), cur(vcol), prev(kcol), prev(vcol),
                  pl.BlockSpec((len(DILATED_PATTERNS), 2, ATT_STEPS, 2 * ATT_STEPS), lambda b, hp, i: (0, hp, 0, 0))],
        out_specs=pl.BlockSpec((1, span, LANES), lambda b, hp, i: (b, i, hp)),
        out_shape=jax.ShapeDtypeStruct((n, s, ATT_WIDTH), BF16),
        scratch_shapes=[pltpu.VMEM((2 * span, LANES), F32), pltpu.VMEM((2 * span, LANES), F32),
                        pltpu.VMEM((len(DILATED_PATTERNS), span, LANES), F32),
                        pltpu.VMEM((len(DILATED_PATTERNS), span, LANES), F32)],
        compiler_params=_cparams(("arbitrary", "arbitrary", "arbitrary")),
        name="att_prompt",
    )(z3, z3, z3, z3, z3, bias)


SAMPLE_KEYS = 136


def _sample_bias(rel_table):
    out = []
    for _, dil in DILATED_PATTERNS:
        steps = ATT_STEPS - np.arange(ATT_STEPS + 1)
        bias = rel_table[_t5_bucket(steps * dil)].astype(F32)
        pad = jnp.full((SAMPLE_KEYS - ATT_STEPS - 1, ATT_HEADS), -jnp.inf, F32)
        out.append(jnp.concatenate([bias, pad], axis=0))
    return jnp.stack(out)


def _att_sample_kernel(q_ref, kn_ref, vn_ref, ck_ref, cv_ref, tb_ref, o_ref, kall, vall, *, win, t_new):
    n_lt = ATT_WIDTH // LANES
    pad = kall.shape[1] - win - t_new
    for c in range(n_lt):
        lanes = slice(LANES * c, LANES * (c + 1))
        kall[c, 0:win, :] = ck_ref[0, :, lanes]
        kall[c, win:win + t_new, :] = kn_ref[0, :, lanes]
        kall[c, win + t_new:, :] = jnp.zeros((pad, LANES), F32)
        vall[c, 0:win, :] = cv_ref[0, :, lanes]
        vall[c, win:win + t_new, :] = vn_ref[0, :, lanes]
        vall[c, win + t_new:, :] = jnp.zeros((pad, LANES), F32)

    def gather_rows(ref, rows):
        return jnp.concatenate([ref[c, rows, :] for c in range(n_lt)], axis=-1)

    own_head = (lax.broadcasted_iota(jnp.int32, (ATT_HEADS, ATT_WIDTH), 1) // ATT_HEAD_DIM
                == lax.broadcasted_iota(jnp.int32, (ATT_HEADS, ATT_WIDTH), 0))
    expand = own_head.astype(BF16)
    for t in range(t_new):
        q_t = jnp.broadcast_to(q_ref[0, t:t + 1, :] * (ATT_HEAD_DIM ** -0.5), (ATT_HEADS, ATT_WIDTH))
        q_heads = jnp.where(own_head, q_t, 0.0).astype(BF16)
        outs, lses = [], []
        for pi, (_, dil) in enumerate(DILATED_PATTERNS):
            start = win + t - ATT_STEPS * dil
            rows = pl.ds(start, SAMPLE_KEYS) if dil == 1 else pl.ds(start, SAMPLE_KEYS, stride=dil)
            s = _dot_nt(gather_rows(kall, rows).astype(BF16), q_heads) + tb_ref[pi]
            m = jnp.max(s, axis=0, keepdims=True)
            e = jnp.exp(s - m)
            l = jnp.sum(e, axis=0, keepdims=True)
            p_lanes = _dot((e / l).astype(BF16), expand)
            v = gather_rows(vall, rows).astype(BF16).astype(F32)
            outs.append(jnp.sum(p_lanes * v, axis=0, keepdims=True))
            lses.append(m + jnp.log(l))
        pad_rows = jnp.full((ATT_HEADS - len(lses), ATT_HEADS), -jnp.inf, F32)
        lse = jnp.concatenate(lses + [pad_rows], axis=0)
        w = jnp.exp(lse - jnp.max(lse, axis=0, keepdims=True))
        w_lanes = _dot((w / jnp.sum(w, axis=0, keepdims=True)).astype(BF16), expand)
        out = None
        for pi in range(len(DILATED_PATTERNS)):
            term = w_lanes[pi:pi + 1, :] * outs[pi].astype(BF16).astype(F32)
            out = term if out is None else out + term
        o_ref[0, t:t + 1, :] = out


def _att_sample(z3, cache_k, cache_v, tb):
    n, t_new, _ = z3.shape
    win = cache_k.shape[1]
    assert win >= ATT_STEPS * DILATED_PATTERNS[-1][1], "every key of the widest pattern must be inside the window buffer"
    pad = (SAMPLE_KEYS - ATT_STEPS - 1) * DILATED_PATTERNS[-1][1]
    col = lambda c: pl.BlockSpec((1, t_new, ATT_WIDTH), lambda b: (b, 0, c))
    return pl.pallas_call(
        functools.partial(_att_sample_kernel, win=win, t_new=t_new),
        grid=(n,),
        in_specs=[col(1), col(2), col(3),
                  pl.BlockSpec((1, win, ATT_WIDTH), lambda b: (b, 0, 0)),
                  pl.BlockSpec((1, win, ATT_WIDTH), lambda b: (b, 0, 0)),
                  pl.BlockSpec((len(DILATED_PATTERNS), SAMPLE_KEYS, ATT_HEADS), lambda b: (0, 0, 0))],
        out_specs=pl.BlockSpec((1, t_new, ATT_WIDTH), lambda b: (b, 0, 0)),
        out_shape=jax.ShapeDtypeStruct((n, t_new, ATT_WIDTH), F32),
        scratch_shapes=[pltpu.VMEM((ATT_WIDTH // LANES, win + t_new + pad, LANES), F32),
                        pltpu.VMEM((ATT_WIDTH // LANES, win + t_new + pad, LANES), F32)],
        compiler_params=_cparams(("arbitrary",)),
        name="att_sample",
    )(z3, z3, z3, cache_k, cache_v, tb)


def _ret_tables(chunk, t, pos0):
    f32 = F32
    log_g = jnp.log(1.0 - 2.0 ** (-5.0 - jnp.arange(RET_HEADS, dtype=f32)))
    i = jnp.arange(chunk, dtype=f32)
    rel = i[:, None] - i[None, :]
    intra = jnp.where(rel >= 0, jnp.exp(jnp.maximum(rel, 0.0)[None] * log_g[:, None, None]), 0.0)
    q_decay = jnp.exp((i + 1.0)[None, :] * log_g[:, None])[..., None]
    k_decay = jnp.exp((chunk - 1.0 - i)[None, :] * log_g[:, None])[..., None]
    chunk_decay = jnp.exp(chunk * log_g)[:, None, None]
    half = RET_KEY_DIM // 2
    inv = ROPE_BASE ** (-jnp.arange(half, dtype=f32) / half)
    pos = pos0 + jnp.arange(t, dtype=f32)
    ang = pos[:, None] * inv[None, :]
    return intra, q_decay, k_decay, chunk_decay, jnp.cos(ang), jnp.sin(ang)


def _ret_kernel(*refs, has_s0):
    if has_s0:
        (q_ref, k_ref, v_ref, g_ref, cos_ref, sin_ref, dec_ref, qd_ref, kd_ref, cd_ref, s0_ref,
         o_ref, sf_ref, s_scr) = refs
    else:
        (q_ref, k_ref, v_ref, g_ref, cos_ref, sin_ref, dec_ref, qd_ref, kd_ref, cd_ref,
         o_ref, sf_ref, s_scr) = refs
    half = RET_KEY_DIM // 2

    @pl.when(pl.program_id(2) == 0)
    def _():
        if has_s0:
            s_scr[...] = s0_ref[0, 0]
        else:
            s_scr[...] = jnp.zeros(s_scr.shape, F32)

    cos = cos_ref[...]
    sin = sin_ref[...]

    def rope(x):
        x1, x2 = x[:, :half], x[:, half:]
        return jnp.concatenate([x1 * cos - x2 * sin, x1 * sin + x2 * cos], axis=-1)

    q = rope(q_ref[0])
    k = rope(k_ref[0]) * (RET_KEY_DIM ** -0.5)
    qb = q.astype(BF16)
    vb = v_ref[0].astype(BF16)
    scores = _dot_nt(qb, k.astype(BF16)) * dec_ref[0]
    s_old = s_scr[...]
    o = _dot(scores.astype(BF16), vb) + _dot(qb, s_old.astype(BF16)) * qd_ref[0]
    s_new = cd_ref[0] * s_old + _dot_tn((k * kd_ref[0]).astype(BF16), vb)
    s_scr[...] = s_new
    sf_ref[0, 0] = s_new
    mu = jnp.mean(o, axis=-1, keepdims=True)
    var = jnp.mean(jnp.square(o - mu), axis=-1, keepdims=True)
    o_ref[0] = (jax.nn.silu(g_ref[0]) * ((o - mu) * lax.rsqrt(var + GN_EPS))).astype(o_ref.dtype)


def _retention(zz3, s0, pos0):
    n, t, _ = zz3.shape
    chunk = RET_CHUNK if t % RET_CHUNK == 0 else t
    intra, q_decay, k_decay, chunk_decay, cos, sin = _ret_tables(chunk, t, pos0)
    kq = RET_KEY_DIM
    vq = RET_VAL_DIM
    in_specs = [pl.BlockSpec((1, chunk, kq), lambda b, h, c: (b, c, h)),
                pl.BlockSpec((1, chunk, kq), lambda b, h, c: (b, c, RET_HEADS + h)),
                pl.BlockSpec((1, chunk, vq), lambda b, h, c: (b, c, 2 * D_MODEL // vq + h)),
                pl.BlockSpec((1, chunk, vq), lambda b, h, c: (b, c, (2 * D_MODEL + RET_VAL_WIDTH) // vq + h)),
                pl.BlockSpec((chunk, kq // 2), lambda b, h, c: (c, 0)),
                pl.BlockSpec((chunk, kq // 2), lambda b, h, c: (c, 0)),
                pl.BlockSpec((1, chunk, chunk), lambda b, h, c: (h, 0, 0)),
                pl.BlockSpec((1, chunk, 1), lambda b, h, c: (h, 0, 0)),
                pl.BlockSpec((1, chunk, 1), lambda b, h, c: (h, 0, 0)),
                pl.BlockSpec((1, 1, 1), lambda b, h, c: (h, 0, 0))]
    args = [zz3, zz3, zz3, zz3, cos, sin, intra, q_decay, k_decay, chunk_decay]
    if s0 is not None:
        in_specs.append(pl.BlockSpec((1, 1, kq, vq), lambda b, h, c: (b, h, 0, 0)))
        args.append(s0)
    return pl.pallas_call(
        functools.partial(_ret_kernel, has_s0=s0 is not None),
        grid=(n, RET_HEADS, t // chunk),
        in_specs=in_specs,
        out_specs=[pl.BlockSpec((1, chunk, vq), lambda b, h, c: (b, c, h)),
                   pl.BlockSpec((1, 1, kq, vq), lambda b, h, c: (b, h, 0, 0))],
        out_shape=[jax.ShapeDtypeStruct((n, t, RET_VAL_WIDTH), F32),
                   jax.ShapeDtypeStruct((n, RET_HEADS, kq, vq), F32)],
        scratch_shapes=[pltpu.VMEM((kq, vq), F32)],
        compiler_params=_cparams(("arbitrary", "arbitrary", "arbitrary")),
        name="retention",
    )(*args)


def _router_kernel(x_ref, sh_ref, sc_ref, g_ref, rw_ref, rb_ref, h_ref, idx_ref, wt_ref, *, seq_mod):
    h = _rms(x_ref[...], g_ref[...]) * (1.0 + _mod_rows(sc_ref, seq_mod)) + _mod_rows(sh_ref, seq_mod)
    h_ref[...] = h
    logits = _dot(h.astype(BF16), rw_ref[...].astype(BF16)) + rb_ref[...]
    lane = lax.broadcasted_iota(jnp.int32, logits.shape, 1)
    m1 = jnp.max(logits, axis=-1, keepdims=True)
    i1 = jnp.min(jnp.where(logits == m1, lane, N_EXPERTS), axis=-1, keepdims=True)
    rest = jnp.where(lane == i1, -jnp.inf, logits)
    m2 = jnp.max(rest, axis=-1, keepdims=True)
    i2 = jnp.min(jnp.where(rest == m2, lane, N_EXPERTS), axis=-1, keepdims=True)
    e2 = jnp.exp(m2 - m1)
    idx_ref[:, 0:1] = i1
    idx_ref[:, 1:2] = i2
    wt_ref[:, 0:1] = 1.0 / (1.0 + e2)
    wt_ref[:, 1:2] = e2 / (1.0 + e2)


def _router(x, shift, scale, g_pre, rw, rb, *, seq_mod, rows_per_seq, tm):
    t, d = x.shape
    ms = _mod_spec(seq_mod, tm, rows_per_seq, 1)
    return pl.pallas_call(
        functools.partial(_router_kernel, seq_mod=seq_mod),
        grid=(t // tm,),
        in_specs=[pl.BlockSpec((tm, d), lambda i: (i, 0)), ms, ms,
                  pl.BlockSpec((1, d), lambda i: (0, 0)),
                  pl.BlockSpec((d, N_EXPERTS), lambda i: (0, 0)),
                  pl.BlockSpec((1, N_EXPERTS), lambda i: (0, 0))],
        out_specs=[pl.BlockSpec((tm, d), lambda i: (i, 0)),
                   pl.BlockSpec((tm, TOP_K), lambda i: (i, 0)),
                   pl.BlockSpec((tm, TOP_K), lambda i: (i, 0))],
        out_shape=[jax.ShapeDtypeStruct((t, d), F32),
                   jax.ShapeDtypeStruct((t, TOP_K), jnp.int32),
                   jax.ShapeDtypeStruct((t, TOP_K), F32)],
        compiler_params=_cparams(("arbitrary",)),
        name="router",
    )(x, shift, scale, g_pre.reshape(1, d), rw, rb.reshape(1, N_EXPERTS).astype(F32))


def _dispatch_kernel(dest_ref, h_ref, xs_in_ref, xs_ref, sem, *, tm):
    del xs_in_ref

    def row_copy(r, slot):
        return pltpu.make_async_copy(h_ref.at[pl.ds(r, 1)], xs_ref.at[pl.ds(dest_ref[TOP_K * r + slot], 1)], sem)

    def issue(r, carry):
        for slot in range(TOP_K):
            row_copy(r, slot).start()
        return carry

    lax.fori_loop(0, tm, issue, 0)

    def drain(r, carry):
        for slot in range(TOP_K):
            row_copy(r, slot).wait()
        return carry

    lax.fori_loop(0, tm, drain, 0)


def _dispatch(dest, h, xs, *, tm):
    t, d = h.shape
    return pl.pallas_call(
        functools.partial(_dispatch_kernel, tm=tm),
        grid=(t // tm,),
        in_specs=[pl.BlockSpec((TOP_K * tm,), lambda i: (i,), memory_space=pltpu.SMEM),
                  pl.BlockSpec((tm, d), lambda i: (i, 0)),
                  pl.BlockSpec(memory_space=pl.ANY)],
        out_specs=pl.BlockSpec(memory_space=pl.ANY),
        out_shape=jax.ShapeDtypeStruct(xs.shape, xs.dtype),
        scratch_shapes=[pltpu.SemaphoreType.DMA(())],
        input_output_aliases={2: 0},
        compiler_params=_cparams(("arbitrary",)),
        name="moe_dispatch",
    )(dest, h, xs)


def _experts_kernel(te_ref, ns_ref, x_ref, w1_ref, w3_ref, w2_ref, o_ref, w1_s, w3_s, w2_s):
    w = pl.program_id(0)
    j = pl.program_id(1)
    nsub = ns_ref[w]

    @pl.when(nsub > 0)
    def _():
        w1_s[...] = w1_ref[0].astype(BF16)
        w3_s[...] = w3_ref[0].astype(BF16)
        w2_s[...] = w2_ref[0].astype(BF16)

    for s in range(MOE_TILE // MOE_SUB):
        rows = pl.ds(s * MOE_SUB, MOE_SUB)

        @pl.when(s < nsub)
        def _():
            xs = x_ref[rows, :].astype(BF16)
            a = jax.nn.silu(_dot(xs, w1_s[...])) * _dot(xs, w3_s[...])
            part = _dot(a.astype(BF16), w2_s[...])

            @pl.when(j == 0)
            def _():
                o_ref[rows, :] = part

            @pl.when(j > 0)
            def _():
                o_ref[rows, :] += part

        @pl.when(jnp.logical_and(s >= nsub, j == 0))
        def _():
            o_ref[rows, :] = jnp.zeros((MOE_SUB, D_MODEL), F32)


def _experts(tile_expert, tile_nsub, xs, w1, w3, w2):
    r, d = xs.shape
    f = w1.shape[2]
    nj = f // MOE_TN
    jj = lambda w, j, ns: jnp.where(ns[w] > 0, j, nj - 1)
    grid_spec = pltpu.PrefetchScalarGridSpec(
        num_scalar_prefetch=2,
        grid=(r // MOE_TILE, nj),
        in_specs=[pl.BlockSpec((MOE_TILE, d), lambda w, j, te, ns: (w, 0)),
                  pl.BlockSpec((1, d, MOE_TN), lambda w, j, te, ns: (te[w], 0, jj(w, j, ns))),
                  pl.BlockSpec((1, d, MOE_TN), lambda w, j, te, ns: (te[w], 0, jj(w, j, ns))),
                  pl.BlockSpec((1, MOE_TN, d), lambda w, j, te, ns: (te[w], jj(w, j, ns), 0))],
        out_specs=pl.BlockSpec((MOE_TILE, d), lambda w, j, te, ns: (w, 0)),
        scratch_shapes=[pltpu.VMEM((d, MOE_TN), BF16), pltpu.VMEM((d, MOE_TN), BF16),
                        pltpu.VMEM((MOE_TN, d), BF16)],
    )
    return pl.pallas_call(
        _experts_kernel,
        grid_spec=grid_spec,
        out_shape=jax.ShapeDtypeStruct((r, d), F32),
        compiler_params=_cparams(("arbitrary", "arbitrary")),
        name="moe_experts",
    )(tile_expert, tile_nsub, xs, w1, w3, w2)


def _combine_kernel(dest_ref, ys_ref, wt_ref, x_ref, gt_ref, g_ref, o_ref, buf, sem, *, tm, seq_mod):
    def row_copy(r, slot):
        return pltpu.make_async_copy(ys_ref.at[pl.ds(dest_ref[TOP_K * r + slot], 1)],
                                     buf.at[slot, pl.ds(r, 1)], sem)

    def issue(r, carry):
        for slot in range(TOP_K):
            row_copy(r, slot).start()
        return carry

    lax.fori_loop(0, tm, issue, 0)

    def drain(r, carry):
        for slot in range(TOP_K):
            row_copy(r, slot).wait()
        return carry

    lax.fori_loop(0, tm, drain, 0)

    wt = wt_ref[...]
    y = wt[:, 0:1] * buf[0] + wt[:, 1:2] * buf[1]
    o_ref[...] = x_ref[...] + _mod_rows(gt_ref, seq_mod) * _rms(y, g_ref[...])


def _combine(dest, ys, wt, x, gate, g_post, *, seq_mod, rows_per_seq, tm):
    t, d = x.shape
    return pl.pallas_call(
        functools.partial(_combine_kernel, tm=tm, seq_mod=seq_mod),
        grid=(t // tm,),
        in_specs=[pl.BlockSpec((TOP_K * tm,), lambda i: (i,), memory_space=pltpu.SMEM),
                  pl.BlockSpec(memory_space=pl.ANY),
                  pl.BlockSpec((tm, TOP_K), lambda i: (i, 0)),
                  pl.BlockSpec((tm, d), lambda i: (i, 0)),
                  _mod_spec(seq_mod, tm, rows_per_seq, 1),
                  pl.BlockSpec((1, d), lambda i: (0, 0))],
        out_specs=pl.BlockSpec((tm, d), lambda i: (i, 0)),
        out_shape=jax.ShapeDtypeStruct((t, d), F32),
        scratch_shapes=[pltpu.VMEM((TOP_K, tm, d), F32), pltpu.SemaphoreType.DMA(())],
        compiler_params=_cparams(("arbitrary",)),
        name="moe_combine",
    )(dest, ys, wt, x, gate, g_post.reshape(1, d))


def _dispatch_plan(idx_flat):
    pairs = idx_flat.shape[0]
    n_tiles = pairs // MOE_TILE + N_EXPERTS
    onehot = (idx_flat[:, None] == jnp.arange(N_EXPERTS, dtype=jnp.int32)[None, :]).astype(jnp.int32)
    csum = jnp.cumsum(onehot, axis=0)
    rank = jnp.sum(onehot * (csum - 1), axis=1)
    counts = csum[-1]
    tiles_e = (counts + MOE_TILE - 1) // MOE_TILE
    tile_end = jnp.cumsum(tiles_e)
    tile_off = tile_end - tiles_e
    dest = (jnp.sum(onehot * tile_off[None, :], axis=1) * MOE_TILE + rank).astype(jnp.int32)
    w = jnp.arange(n_tiles, dtype=jnp.int32)
    used = w < tile_end[-1]
    e_of = jnp.minimum(jnp.sum((w[:, None] >= tile_end[None, :]).astype(jnp.int32), axis=1), N_EXPERTS - 1)
    last_e = e_of[jnp.maximum(tile_end[-1] - 1, 0)]
    rows_left = counts[e_of] - (w - tile_off[e_of]) * MOE_TILE
    nsub = jnp.clip((rows_left + MOE_SUB - 1) // MOE_SUB, 0, MOE_TILE // MOE_SUB)
    tile_expert = jnp.where(used, e_of, last_e).astype(jnp.int32)
    tile_nsub = jnp.where(used, nsub, 0).astype(jnp.int32)
    return dest, tile_expert, tile_nsub, n_tiles


def kernel(x_prompt, x_sample, cache_k_win, cache_v_win, state_ssm_re, state_ssm_im, state_ret, c_prompt, c_sample, rel_bias_table, l0_mix_mod_w, l0_mix_mod_b, l0_mix_norm_pre, l0_mix_norm_post, l0_w_in, l0_w_out, l0_ssm_a_re, l0_ssm_a_im, l0_ssm_log_dt, l0_ssm_b_re, l0_ssm_b_im, l0_ssm_c_re, l0_ssm_c_im, l0_ssm_d, l0_glu_w, l0_glu_b, l0_ffn_mod_w, l0_ffn_mod_b, l0_ffn_norm_pre, l0_ffn_norm_post, l0_ffn_w1, l0_ffn_w3, l0_ffn_w2, l1_mix_mod_w, l1_mix_mod_b, l1_mix_norm_pre, l1_mix_norm_post, l1_w_in, l1_w_out, l1_ffn_mod_w, l1_ffn_mod_b, l1_ffn_norm_pre, l1_ffn_norm_post, l1_router_w, l1_router_b, l1_moe_w1, l1_moe_w3, l1_moe_w2):
    nb_p, seq, d = x_prompt.shape
    nb_s, t_s, _ = x_sample.shape
    tp = nb_p * seq
    ts = nb_s * t_s
    xp = x_prompt.reshape(tp, d)
    xs = x_sample.reshape(ts, d)
    tm_p = 1024

    n_c = nb_p + nb_s
    c_all = jnp.concatenate([c_prompt, c_sample, jnp.zeros((-n_c % 8, d), F32)], axis=0)

    def mods(w, b):
        m = _adaln(c_all, w, b)
        parts = []
        for a in jnp.split(m, 3, axis=-1):
            parts.append((a[:nb_p].reshape(nb_p, 1, d), jnp.repeat(a[nb_p:n_c], t_s, axis=0)))
        return parts

    mod_mix0 = mods(l0_mix_mod_w, l0_mix_mod_b)
    mod_ffn0 = mods(l0_ffn_mod_w, l0_ffn_mod_b)
    mod_mix1 = mods(l1_mix_mod_w, l1_mix_mod_b)
    mod_ffn1 = mods(l1_ffn_mod_w, l1_ffn_mod_b)

    kw_p = dict(seq_mod=True, rows_per_seq=seq, tm=tm_p)
    kw_s = dict(seq_mod=False, rows_per_seq=t_s, tm=ts)

    (sh_p, sh_s), (sc_p, sc_s), (gt_p, gt_s) = mod_mix0
    z_p = _lin_in(xp, sh_p, sc_p, l0_mix_norm_pre, l0_w_in, tn=1024, **kw_p)
    z_s = _lin_in(xs, sh_s, sc_s, l0_mix_norm_pre, l0_w_in, tn=1024, **kw_s)
    z_p3 = z_p.reshape(nb_p, seq, -1)
    z_s3 = z_s.reshape(nb_s, t_s, -1)

    abar, bw, cw = _s5_weights(l0_ssm_a_re, l0_ssm_a_im, l0_ssm_log_dt, l0_ssm_b_re, l0_ssm_b_im,
                               l0_ssm_c_re, l0_ssm_c_im)
    zero_slab = jnp.zeros((nb_p, SLAB_ROWS, SLAB_HALF), F32)
    ssm_p, sre_p, sim_p = _s5(z_p3, zero_slab, zero_slab, abar, bw, cw, l0_ssm_d, l0_glu_w, l0_glu_b,
                              nb=nb_p, tb=256)
    ssm_s, sre_s, sim_s = _s5(z_s3, state_ssm_re.astype(F32).reshape(nb_s, SLAB_ROWS, SLAB_HALF),
                              state_ssm_im.astype(F32).reshape(nb_s, SLAB_ROWS, SLAB_HALF),
                              abar, bw, cw, l0_ssm_d, l0_glu_w, l0_glu_b, nb=8, tb=t_s)

    att_p = _att_prompt(z_p3, _prompt_bias(rel_bias_table), span=WIN_MAX)
    att_s = _att_sample(z_s3, cache_k_win.reshape(nb_s, -1, ATT_WIDTH), cache_v_win.reshape(nb_s, -1, ATT_WIDTH),
                        _sample_bias(rel_bias_table))

    half_spec = lambda r: pl.BlockSpec((SSM_WIDTH, d), lambda i, r=r: (r, 0))
    w_out0 = dict(w_specs=[half_spec(0), half_spec(1)], w_arrays=[l0_w_out, l0_w_out])
    xp = _lin_out([ssm_p.reshape(tp, -1), att_p.reshape(tp, -1)], x=xp, gate=gt_p, g_post=l0_mix_norm_post,
                  **w_out0, **kw_p)
    xs = _lin_out([ssm_s.reshape(ts, -1), att_s.reshape(ts, -1)], x=xs, gate=gt_s, g_post=l0_mix_norm_post,
                  **w_out0, **kw_s)

    keep = min(WIN_MAX, seq)
    k_win_p = z_p3[:, seq - keep:, 2 * ATT_WIDTH:3 * ATT_WIDTH].reshape(nb_p, keep, ATT_HEADS, ATT_HEAD_DIM)
    v_win_p = z_p3[:, seq - keep:, 3 * ATT_WIDTH:].reshape(nb_p, keep, ATT_HEADS, ATT_HEAD_DIM)
    k_new_s = z_s3[:, :, 2 * ATT_WIDTH:3 * ATT_WIDTH].reshape(nb_s, t_s, ATT_HEADS, ATT_HEAD_DIM)
    v_new_s = z_s3[:, :, 3 * ATT_WIDTH:].reshape(nb_s, t_s, ATT_HEADS, ATT_HEAD_DIM)

    (sh_p, sh_s), (sc_p, sc_s), (gt_p, gt_s) = mod_ffn0
    ffn_w = (l0_ffn_norm_pre, l0_ffn_norm_post, l0_ffn_w1, l0_ffn_w3, l0_ffn_w2)
    xp = _ffn(xp, sh_p, sc_p, gt_p, *ffn_w, tn=256, **kw_p)
    xs = _ffn(xs, sh_s, sc_s, gt_s, *ffn_w, tn=256, **kw_s)

    (sh_p, sh_s), (sc_p, sc_s), (gt_p, gt_s) = mod_mix1
    zz_p = _lin_in(xp, sh_p, sc_p, l1_mix_norm_pre, l1_w_in, tn=1024, **kw_p)
    zz_s = _lin_in(xs, sh_s, sc_s, l1_mix_norm_pre, l1_w_in, tn=1024, **kw_s)
    og_p, ret_p = _retention(zz_p.reshape(nb_p, seq, -1), None, 0)
    og_s, ret_s = _retention(zz_s.reshape(nb_s, t_s, -1), state_ret, PAST_POS)
    w_out1 = dict(w_specs=[pl.BlockSpec((RET_VAL_WIDTH, d), lambda i: (0, 0))], w_arrays=[l1_w_out])
    xp = _lin_out([og_p.reshape(tp, -1)], x=xp, gate=gt_p, g_post=l1_mix_norm_post, **w_out1, **kw_p)
    xs = _lin_out([og_s.reshape(ts, -1)], x=xs, gate=gt_s, g_post=l1_mix_norm_post, **w_out1, **kw_s)

    (sh_p, sh_s), (sc_p, sc_s), (gt_p, gt_s) = mod_ffn1
    h_p, idx_p, wt_p = _router(xp, sh_p, sc_p, l1_ffn_norm_pre, l1_router_w, l1_router_b, **kw_p)
    h_s, idx_s, wt_s = _router(xs, sh_s, sc_s, l1_ffn_norm_pre, l1_router_w, l1_router_b, **kw_s)
    idx_flat = jnp.concatenate([idx_p.reshape(-1), idx_s.reshape(-1)])
    dest, tile_expert, tile_nsub, n_tiles = _dispatch_plan(idx_flat)
    dest_p, dest_s = dest[:TOP_K * tp], dest[TOP_K * tp:]
    tm_d = 256
    x_sorted = jnp.zeros((n_tiles * MOE_TILE, d), F32)
    x_sorted = _dispatch(dest_p, h_p, x_sorted, tm=tm_d)
    x_sorted = _dispatch(dest_s, h_s, x_sorted, tm=tm_d)
    y_sorted = _experts(tile_expert, tile_nsub, x_sorted, l1_moe_w1, l1_moe_w3, l1_moe_w2)
    xp = _combine(dest_p, y_sorted, wt_p, xp, gt_p, l1_ffn_norm_post, seq_mod=True, rows_per_seq=seq, tm=tm_d)
    xs = _combine(dest_s, y_sorted, wt_s, xs, gt_s, l1_ffn_norm_post, seq_mod=False, rows_per_seq=t_s, tm=tm_d)

    grp = (SSM_GROUPS, SSM_STATE)
    return (xp.reshape(nb_p, seq, d), xs.reshape(nb_s, t_s, d), k_win_p, v_win_p,
            sre_p.reshape(nb_p, *grp), sim_p.reshape(nb_p, *grp), ret_p,
            k_new_s, v_new_s, sre_s.reshape(nb_s, *grp), sim_s.reshape(nb_s, *grp), ret_s)
```
